```python
import math
import jax
import jax.numpy as jnp
from jax import lax
import numpy as np

D_MODEL = 1024
BATCH = 16
SEQ = 256
DEPTH = 4
DEC_BATCH = 2
DEC_SEQ = 4096
PAST_LEN = 256

GRID_W = 64
N_AB = (DEPTH + 1) // 2
N_CD = DEPTH // 2
CHUNK = 64
NORM_EPS = 1e-6
POS_BASE = 10000.0
CONV_W = 4
CONV_PAD_LO = 1
GLA_HEADS = 4
GLA_DV = D_MODEL // 8
GLA_DK = GLA_DV // 2
GLA_RANK = 16
GLA_TAU = 16.0
RG_WIDTH = D_MODEL // 2
RG_BLOCKS = 8
RG_BLOCK = RG_WIDTH // RG_BLOCKS
RG_C = 8.0
S5_WIDTH = D_MODEL // 2
S5_GROUP = 16
S5_GROUPS = S5_WIDTH // S5_GROUP
S5_STATE = 64
DN_HEADS = 4
DN_DK = D_MODEL // 8
DN_DV = D_MODEL // 8
DN_CONV_CH = 2 * DN_HEADS * DN_DK + DN_HEADS * DN_DV
AB_IN = 2 * GLA_HEADS * GLA_DK + 2 * GLA_HEADS * GLA_DV + 2 * GLA_RANK + 2 * RG_WIDTH
AB_OUT = GLA_HEADS * GLA_DV + RG_WIDTH
CD_IN = S5_WIDTH + DN_CONV_CH + DN_HEADS * DN_DV + 4 * DN_HEADS
CD_OUT = S5_WIDTH + DN_HEADS * DN_DV
D_FF = -(-8 * D_MODEL // (3 * 256)) * 256

kernel_name = 'hybrid_bidir_recurrent_diffusion_step'


def _split(x, sizes):
    parts, start = [], 0
    for s in sizes:
        parts.append(x[..., start:start + s])
        start += s
    return parts


def _flip(x):
    return jnp.flip(x, axis=1)


def rmsnorm(x, w):
    xf = x.astype(jnp.float32)
    y = xf * lax.rsqrt(jnp.mean(xf * xf, axis=-1, keepdims=True) + NORM_EPS)
    return (y * w.astype(jnp.float32)).astype(x.dtype)


def _head_rms(x, w):
    return x * lax.rsqrt(jnp.mean(x * x, axis=-1, keepdims=True) + NORM_EPS) * w.astype(jnp.float32)


def _l2norm(x):
    return x * lax.rsqrt(jnp.sum(x * x, axis=-1, keepdims=True) + 1e-6)


def centred_dwconv(x, w, b=None):
    n_tok = x.shape[1]
    xp = jnp.pad(x, ((0, 0), (CONV_PAD_LO, CONV_W - 1 - CONV_PAD_LO), (0, 0)))
    y = xp[:, 0:n_tok] * w[0]
    for j in range(1, CONV_W):
        y = y + xp[:, j:j + n_tok] * w[j]
    return y if b is None else y + b


def diag_scan(a, b, h0):
    def comb(left, right):
        a_l, b_l = left
        a_r, b_r = right
        return a_r * a_l, a_r * b_l + b_r
    a_cum, h = lax.associative_scan(comb, (a, b), axis=1)
    return h + a_cum * h0[:, None]


def gla_chunked(q, k, v, log_a, s0):
    bsz, n_tok, n_h, _ = q.shape
    dv = v.shape[-1]
    n_chunks = n_tok // CHUNK
    blk = lambda t: jnp.moveaxis(t.reshape((bsz, n_chunks, CHUNK) + t.shape[2:]), 2, 3)
    q, k, v, log_a = blk(q), blk(k), blk(v), blk(log_a)
    b = jnp.cumsum(log_a, axis=3)
    b_end = b[:, :, :, -1:]
    q_dec = q * jnp.exp(b)
    k_inv = k * jnp.exp(-b)
    k_end = k * jnp.exp(b_end - b)
    causal = jnp.tril(jnp.ones((CHUNK, CHUNK), bool))
    att = jnp.where(causal, jnp.einsum('bnhtk,bnhsk->bnhts', q_dec, k_inv), 0.0)
    o_intra = jnp.einsum('bnhts,bnhsv->bnhtv', att, v)
    kv = jnp.einsum('bnhsk,bnhsv->bnhkv', k_end, v)
    dec_end = jnp.exp(b_end[:, :, :, 0])

    def step(state, inp):
        d, kv_c = inp
        return d[..., None] * state + kv_c, state

    s_fin, s_prev = lax.scan(step, s0, (jnp.moveaxis(dec_end, 1, 0), jnp.moveaxis(kv, 1, 0)))
    o_inter = jnp.einsum('bnhtk,nbhkv->bnhtv', q_dec, s_prev)
    o = jnp.transpose(o_intra + o_inter, (0, 1, 3, 2, 4)).reshape(bsz, n_tok, n_h, dv)
    return o, s_fin


def delta_chunked(q, k, v, beta, log_alpha, s0):
    bsz, n_tok, n_h, _ = q.shape
    dv = v.shape[-1]
    n_chunks = n_tok // CHUNK
    blk = lambda t: jnp.moveaxis(t.reshape((bsz, n_chunks, CHUNK) + t.shape[2:]), 2, 3)
    q, k, v, beta, log_alpha = blk(q), blk(k), blk(v), blk(beta), blk(log_alpha)
    g = jnp.cumsum(log_alpha, axis=-1)
    diff = g[..., :, None] - g[..., None, :]
    causal = jnp.tril(jnp.ones((CHUNK, CHUNK), bool))
    strict = jnp.tril(jnp.ones((CHUNK, CHUNK), bool), -1)
    dec = jnp.exp(jnp.where(causal, diff, -jnp.inf))
    kk = jnp.einsum('bnhtk,bnhsk->bnhts', k, k)
    lhs = jnp.eye(CHUNK, dtype=q.dtype) + jnp.where(strict, beta[..., :, None] * kk * dec, 0.0)
    rhs = jnp.concatenate([beta[..., None] * v, (beta * jnp.exp(g))[..., None] * k], axis=-1)
    sol = lax.linalg.triangular_solve(lhs, rhs, left_side=True, lower=True, unit_diagonal=True)
    u_pre, w_pre = sol[..., :dv], sol[..., dv:]
    qk = jnp.einsum('bnhtk,bnhsk->bnhts', q, k) * dec
    q_dec = q * jnp.exp(g)[..., None]
    k_end = k * jnp.exp(g[..., -1:] - g)[..., None]
    g_end = jnp.exp(g[..., -1])

    def step(state, inp):
        u_c, w_c, qk_c, qd_c, ke_c, ge_c = inp
        w_new = u_c - jnp.einsum('bhck,bhkv->bhcv', w_c, state)
        o = jnp.einsum('bhck,bhkv->bhcv', qd_c, state) + jnp.einsum('bhts,bhsv->bhtv', qk_c, w_new)
        state = ge_c[..., None, None] * state + jnp.einsum('bhck,bhcv->bhkv', ke_c, w_new)
        return state, o

    xs = tuple(jnp.moveaxis(t, 1, 0) for t in (u_pre, w_pre, qk, q_dec, k_end, g_end))
    s_fin, o = lax.scan(step, s0, xs)
    o = jnp.transpose(o, (1, 0, 3, 2, 4)).reshape(bsz, n_tok, n_h, dv)
    return o, s_fin


def rglru_scan(x, wa, ba, wx, bx, lam, h0):
    bsz, n_tok, width = x.shape
    xb = x.reshape(bsz, n_tok, RG_BLOCKS, RG_BLOCK)
    r = jax.nn.sigmoid(jnp.einsum('blnj,njk->blnk', xb, wa).reshape(bsz, n_tok, width) + ba)
    i = jax.nn.sigmoid(jnp.einsum('blnj,njk->blnk', xb, wx).reshape(bsz, n_tok, width) + bx)
    log_a = -RG_C * r * jax.nn.softplus(-lam)
    a = jnp.exp(log_a)
    b = jnp.sqrt(-jnp.expm1(2.0 * log_a)) * (i * x)
    h = diag_scan(a, b, h0)
    return h, h[:, -1]


def s5_scan(u, lam_re, lam_im, log_dt, b_re, b_im, c_re, c_im, h0):
    lam = lax.complex(lam_re, lam_im)
    a_bar = jnp.exp(lam * jnp.exp(log_dt)[:, None])
    b_bar = ((a_bar - 1.0) / lam)[..., None] * lax.complex(b_re, b_im)
    bu = jnp.einsum('gpj,blgj->blgp', b_bar, u.astype(jnp.complex64))
    h = diag_scan(jnp.broadcast_to(a_bar, bu.shape), bu, h0)
    y = jnp.einsum('gjp,blgp->blgj', lax.complex(c_re, c_im), h).real
    return y, h[:, -1]


def mixer_ab(h, p, j, s_gla, s_rg):
    bsz, n_tok, _ = h.shape
    f32 = jnp.float32
    proj = jnp.einsum('bld,de->ble', h, p['ab_w_in'][j]).astype(f32)
    q, k, v, g, lr_f, lr_b, rx, rgate = _split(proj, (GLA_HEADS * GLA_DK, GLA_HEADS * GLA_DK, GLA_HEADS * GLA_DV, GLA_HEADS * GLA_DV, GLA_RANK, GLA_RANK, RG_WIDTH, RG_WIDTH))
    heads = lambda t, d: t.reshape(bsz, n_tok, GLA_HEADS, d)
    q = heads(q, GLA_DK) * GLA_DK ** -0.5
    k = heads(k, GLA_DK)
    v = heads(v, GLA_DV)
    wg2 = p['gla_wg2'][j].astype(f32)
    bg = p['gla_bg'][j].astype(f32)
    la_f = heads(jax.nn.log_sigmoid(lr_f @ wg2[0] + bg[0]), GLA_DK) / GLA_TAU
    la_b = heads(jax.nn.log_sigmoid(lr_b @ wg2[1] + bg[1]), GLA_DK) / GLA_TAU
    s_gla = s_gla.astype(f32)
    o_f, sg_f = gla_chunked(q, k, v, la_f, s_gla[:, 0])
    o_b, sg_b = gla_chunked(_flip(q), _flip(k), _flip(v), _flip(la_b), s_gla[:, 1])
    o_gla = _head_rms(o_f + _flip(o_b), p['gla_norm_w'][j]) * jax.nn.silu(heads(g, GLA_DV))
    xc = centred_dwconv(rx, p['rg_conv_w'][j].astype(f32), p['rg_conv_b'][j].astype(f32))
    wa, ba, wx, bx, lam = (p[name][j].astype(f32) for name in ('rg_wa', 'rg_ba', 'rg_wx', 'rg_bx', 'rg_lam'))
    s_rg = s_rg.astype(f32)
    h_f, sr_f = rglru_scan(xc, wa[0], ba[0], wx[0], bx[0], lam[0], s_rg[:, 0])
    h_b, sr_b = rglru_scan(_flip(xc), wa[1], ba[1], wx[1], bx[1], lam[1], s_rg[:, 1])
    o_rg = jax.nn.gelu(rgate) * (h_f + _flip(h_b))
    mixed = jnp.concatenate([o_gla.reshape(bsz, n_tok, GLA_HEADS * GLA_DV), o_rg], axis=-1).astype(h.dtype)
    out = jnp.einsum('ble,ed->bld', mixed, p['ab_w_out'][j])
    return out, jnp.stack([sg_f, sg_b], axis=1), jnp.stack([sr_f, sr_b], axis=1)


def mixer_cd(h, p, j, s_s5, s_dn):
    bsz, n_tok, _ = h.shape
    f32 = jnp.float32
    proj = jnp.einsum('bld,de->ble', h, p['cd_w_in'][j]).astype(f32)
    u, qkv, z, a_in, b_in = _split(proj, (S5_WIDTH, DN_CONV_CH, DN_HEADS * DN_DV, 2 * DN_HEADS, 2 * DN_HEADS))
    lam_re, lam_im, log_dt, b_re, b_im, c_re, c_im, s5_d, glu_w, glu_b = (p[name][j].astype(f32) for name in ('s5_lam_re', 's5_lam_im', 's5_log_dt', 's5_b_re', 's5_b_im', 's5_c_re', 's5_c_im', 's5_d', 's5_glu_w', 's5_glu_b'))
    ug = u.reshape(bsz, n_tok, S5_GROUPS, S5_GROUP)
    s_s5 = s_s5.astype(f32)
    h0 = lax.complex(s_s5[..., 0], s_s5[..., 1])
    y_f, h5_f = s5_scan(ug, lam_re[0], lam_im[0], log_dt[0], b_re[0], b_im[0], c_re[0], c_im[0], h0[:, 0])
    y_b, h5_b = s5_scan(_flip(ug), lam_re[1], lam_im[1], log_dt[1], b_re[1], b_im[1], c_re[1], c_im[1], h0[:, 1])
    y = (y_f + _flip(y_b)).reshape(bsz, n_tok, S5_WIDTH) + s5_d * u
    zg = jax.nn.gelu(y)
    o_s5 = zg * jax.nn.sigmoid(zg @ glu_w + glu_b)
    h5 = jnp.stack([h5_f, h5_b], axis=1)
    s5_new = jnp.stack([h5.real, h5.imag], axis=-1)
    conv_w, a_log, dt_bias = (p[name][j].astype(f32) for name in ('dn_conv_w', 'dn_a_log', 'dn_dt_bias'))
    qkv = jax.nn.silu(centred_dwconv(qkv, conv_w))
    q, k, v = _split(qkv, (DN_HEADS * DN_DK, DN_HEADS * DN_DK, DN_HEADS * DN_DV))
    q = _l2norm(q.reshape(bsz, n_tok, DN_HEADS, DN_DK)) * DN_DK ** -0.5
    k = _l2norm(k.reshape(bsz, n_tok, DN_HEADS, DN_DK))
    v = v.reshape(bsz, n_tok, DN_HEADS, DN_DV)
    log_alpha = -jnp.exp(a_log) * jax.nn.softplus(a_in.reshape(bsz, n_tok, 2, DN_HEADS) + dt_bias)
    beta = jax.nn.sigmoid(b_in.reshape(bsz, n_tok, 2, DN_HEADS))
    s_dn = s_dn.astype(f32)
    o_f, sd_f = delta_chunked(q, k, v, beta[:, :, 0], log_alpha[:, :, 0], s_dn[:, 0])
    o_b, sd_b = delta_chunked(_flip(q), _flip(k), _flip(v), _flip(beta[:, :, 1]), _flip(log_alpha[:, :, 1]), s_dn[:, 1])
    o_dn = _head_rms(o_f + _flip(o_b), p['dn_norm_w'][j]) * jax.nn.silu(z.reshape(bsz, n_tok, DN_HEADS, DN_DV))
    mixed = jnp.concatenate([o_s5, o_dn.reshape(bsz, n_tok, DN_HEADS * DN_DV)], axis=-1).astype(h.dtype)
    out = jnp.einsum('ble,ed->bld', mixed, p['cd_w_out'][j])
    return out, s5_new, jnp.stack([sd_f, sd_b], axis=1)


def swiglu(x, w_gate, w_up, w_down):
    return jnp.einsum('blf,fd->bld', jax.nn.silu(x @ w_gate) * (x @ w_up), w_down)


def grid_pos_embed(n_tokens, d):
    rows = n_tokens // GRID_W
    r, col = jnp.meshgrid(jnp.arange(rows, dtype=jnp.float32), jnp.arange(GRID_W, dtype=jnp.float32), indexing='ij')
    quarter = d // 4
    freqs = jnp.exp(-math.log(POS_BASE) * jnp.arange(quarter, dtype=jnp.float32) / quarter)
    er = r.reshape(-1, 1) * freqs
    ec = col.reshape(-1, 1) * freqs
    return jnp.concatenate([jnp.sin(er), jnp.cos(er), jnp.sin(ec), jnp.cos(ec)], axis=-1)


def trunk(x, cvec, s_gla, s_rg, s_s5, s_dn, p):
    gla_out, rg_out, s5_out, dn_out = [], [], [], []
    for l in range(DEPTH):
        j = l // 2
        mod = jnp.einsum('bd,de->be', jax.nn.silu(cvec), p['w_mod'][l]) + p['b_mod'][l]
        shift1, scale1, gate1, shift2, scale2, gate2 = jnp.split(mod[:, None, :].astype(x.dtype), 6, axis=-1)
        hn = rmsnorm(x, p['norm1_w'][l]) * (1.0 + scale1) + shift1
        if l % 2 == 0:
            out, sg, sr = mixer_ab(hn, p, j, s_gla[:, j], s_rg[:, j])
            gla_out.append(sg)
            rg_out.append(sr)
        else:
            out, ss, sd = mixer_cd(hn, p, j, s_s5[:, j], s_dn[:, j])
            s5_out.append(ss)
            dn_out.append(sd)
        x = x + gate1 * out
        hn = rmsnorm(x, p['norm2_w'][l]) * (1.0 + scale2) + shift2
        x = x + gate2 * swiglu(hn, p['ffn_w_gate'][l], p['ffn_w_up'][l], p['ffn_w_down'][l])
    y = rmsnorm(x, p['final_norm_w'])
    return y, jnp.stack(gla_out, axis=1), jnp.stack(rg_out, axis=1), jnp.stack(s5_out, axis=1), jnp.stack(dn_out, axis=1)


def setup_inputs(seed: int = 0) -> dict:
    key = jax.random.key(seed)
    keys = iter(jax.random.split(key, 64))
    f32 = jnp.float32

    def nrm(shape, scale=1.0):
        return jax.random.normal(next(keys), shape, f32) * scale

    def unif(shape, lo, hi):
        return jax.random.uniform(next(keys), shape, f32, lo, hi)

    def gain(shape):
        return 1.0 + nrm(shape, 0.01)

    rg_a = unif((N_AB, 2, RG_WIDTH), 0.9, 0.999) ** (1.0 / RG_C)
    dn_dt = jnp.exp(unif((N_CD, 2, DN_HEADS), math.log(1e-3), math.log(1e-1)))
    s5_n = jnp.pi * jnp.arange(S5_STATE, dtype=f32)
    return {
        'x_prompt': nrm((BATCH, SEQ, D_MODEL)),
        'x_sample': nrm((DEC_BATCH, DEC_SEQ, D_MODEL)),
        'state_gla': nrm((DEC_BATCH, N_AB, 2, GLA_HEADS, GLA_DK, GLA_DV), 0.3),
        'state_rglru': nrm((DEC_BATCH, N_AB, 2, RG_WIDTH), 0.5),
        'state_s5': nrm((DEC_BATCH, N_CD, 2, S5_GROUPS, S5_STATE, 2), 0.1),
        'state_delta': nrm((DEC_BATCH, N_CD, 2, DN_HEADS, DN_DK, DN_DV), 0.1),
        'c': nrm((DEC_BATCH, D_MODEL)),
        'c_ctx': nrm((D_MODEL,)),
        'norm1_w': gain((DEPTH, D_MODEL)),
        'norm2_w': gain((DEPTH, D_MODEL)),
        'w_mod': nrm((DEPTH, D_MODEL, 6 * D_MODEL), 0.5 * D_MODEL ** -0.5),
        'b_mod': nrm((DEPTH, 6 * D_MODEL), 0.02),
        'ab_w_in': nrm((N_AB, D_MODEL, AB_IN), D_MODEL ** -0.5),
        'ab_w_out': nrm((N_AB, AB_OUT, D_MODEL), AB_OUT ** -0.5),
        'gla_wg2': nrm((N_AB, 2, GLA_RANK, GLA_HEADS * GLA_DK), GLA_RANK ** -0.5),
        'gla_bg': nrm((N_AB, 2, GLA_HEADS * GLA_DK), 0.5),
        'gla_norm_w': gain((N_AB, GLA_DV)),
        'rg_conv_w': nrm((N_AB, CONV_W, RG_WIDTH), CONV_W ** -0.5),
        'rg_conv_b': nrm((N_AB, RG_WIDTH), 0.02),
        'rg_wa': nrm((N_AB, 2, RG_BLOCKS, RG_BLOCK, RG_BLOCK), RG_BLOCK ** -0.5),
        'rg_ba': nrm((N_AB, 2, RG_WIDTH), 0.02),
        'rg_wx': nrm((N_AB, 2, RG_BLOCKS, RG_BLOCK, RG_BLOCK), RG_BLOCK ** -0.5),
        'rg_bx': nrm((N_AB, 2, RG_WIDTH), 0.02),
        'rg_lam': jnp.log(rg_a) - jnp.log1p(-rg_a),
        'cd_w_in': nrm((N_CD, D_MODEL, CD_IN), D_MODEL ** -0.5),
        'cd_w_out': nrm((N_CD, CD_OUT, D_MODEL), CD_OUT ** -0.5),
        's5_lam_re': -0.5 + nrm((N_CD, 2, S5_GROUPS, S5_STATE), 0.01),
        's5_lam_im': s5_n + nrm((N_CD, 2, S5_GROUPS, S5_STATE), 0.01),
        's5_log_dt': unif((N_CD, 2, S5_GROUPS), math.log(1e-3), math.log(1e-1)),
        's5_b_re': nrm((N_CD, 2, S5_GROUPS, S5_STATE, S5_GROUP), (2 * S5_GROUP) ** -0.5),
        's5_b_im': nrm((N_CD, 2, S5_GROUPS, S5_STATE, S5_GROUP), (2 * S5_GROUP) ** -0.5),
        's5_c_re': nrm((N_CD, 2, S5_GROUPS, S5_GROUP, S5_STATE), (2 * S5_STATE) ** -0.5),
        's5_c_im': nrm((N_CD, 2, S5_GROUPS, S5_GROUP, S5_STATE), (2 * S5_STATE) ** -0.5),
        's5_d': nrm((N_CD, S5_WIDTH)),
        's5_glu_w': nrm((N_CD, S5_WIDTH, S5_WIDTH), S5_WIDTH ** -0.5),
        's5_glu_b': nrm((N_CD, S5_WIDTH), 0.02),
        'dn_conv_w': nrm((N_CD, CONV_W, DN_CONV_CH), CONV_W ** -0.5),
        'dn_a_log': jnp.log(unif((N_CD, 2, DN_HEADS), 1.0, 16.0)),
        'dn_dt_bias': dn_dt + jnp.log(-jnp.expm1(-dn_dt)),
        'dn_norm_w': gain((N_CD, DN_DV)),
        'ffn_w_gate': nrm((DEPTH, D_MODEL, D_FF), D_MODEL ** -0.5),
        'ffn_w_up': nrm((DEPTH, D_MODEL, D_FF), D_MODEL ** -0.5),
        'ffn_w_down': nrm((DEPTH, D_FF, D_MODEL), D_FF ** -0.5),
        'final_norm_w': gain((D_MODEL,)),
    }


def reference(x_prompt, x_sample, state_gla, state_rglru, state_s5, state_delta, c, c_ctx,
              norm1_w, norm2_w, w_mod, b_mod, ab_w_in, ab_w_out, gla_wg2, gla_bg, gla_norm_w,
              rg_conv_w, rg_conv_b, rg_wa, rg_ba, rg_wx, rg_bx, rg_lam, cd_w_in, cd_w_out,
              s5_lam_re, s5_lam_im, s5_log_dt, s5_b_re, s5_b_im, s5_c_re, s5_c_im, s5_d, s5_glu_w, s5_glu_b,
              dn_conv_w, dn_a_log, dn_dt_bias, dn_norm_w, ffn_w_gate, ffn_w_up, ffn_w_down, final_norm_w):
    p = {
        'norm1_w': norm1_w, 'norm2_w': norm2_w, 'w_mod': w_mod, 'b_mod': b_mod,
        'ab_w_in': ab_w_in, 'ab_w_out': ab_w_out, 'gla_wg2': gla_wg2, 'gla_bg': gla_bg, 'gla_norm_w': gla_norm_w,
        'rg_conv_w': rg_conv_w, 'rg_conv_b': rg_conv_b, 'rg_wa': rg_wa, 'rg_ba': rg_ba, 'rg_wx': rg_wx,
        'rg_bx': rg_bx, 'rg_lam': rg_lam, 'cd_w_in': cd_w_in, 'cd_w_out': cd_w_out,
        's5_lam_re': s5_lam_re, 's5_lam_im': s5_lam_im, 's5_log_dt': s5_log_dt, 's5_b_re': s5_b_re,
        's5_b_im': s5_b_im, 's5_c_re': s5_c_re, 's5_c_im': s5_c_im, 's5_d': s5_d, 's5_glu_w': s5_glu_w,
        's5_glu_b': s5_glu_b, 'dn_conv_w': dn_conv_w, 'dn_a_log': dn_a_log, 'dn_dt_bias': dn_dt_bias,
        'dn_norm_w': dn_norm_w, 'ffn_w_gate': ffn_w_gate, 'ffn_w_up': ffn_w_up, 'ffn_w_down': ffn_w_down,
        'final_norm_w': final_norm_w,
    }
    n_req = x_prompt.shape[0]
    zeros_like_cache = lambda s: jnp.zeros((n_req,) + s.shape[1:], jnp.float32)
    c_prompt = jnp.broadcast_to(c_ctx, (n_req, c_ctx.shape[0]))
    y_prompt, new_state_gla, new_state_rglru, new_state_s5, new_state_delta = trunk(
        x_prompt, c_prompt, zeros_like_cache(state_gla), zeros_like_cache(state_rglru),
        zeros_like_cache(state_s5), zeros_like_cache(state_delta), p)
    x_lat = x_sample + grid_pos_embed(x_sample.shape[1], x_sample.shape[2]).astype(x_sample.dtype)
    y_sample = trunk(x_lat, c, state_gla, state_rglru, state_s5, state_delta, p)[0]
    return (y_prompt, y_sample, new_state_gla, new_state_rglru, new_state_s5, new_state_delta)
```

```python
import functools
import math

import jax
import jax.numpy as jnp
from jax import lax
from jax.experimental import pallas as pl
from jax.experimental.pallas import tpu as pltpu

F32 = jnp.float32
BF16 = jnp.bfloat16

D_MODEL = 1024
DEPTH = 4
GRID_W = 64
CHUNK = 64
CHUNK_LOG2 = 6
DN_BASE_LOG2 = 3
NORM_EPS = 1e-6
POS_BASE = 10000.0
GLA_HEADS = 4
GLA_DV = 128
GLA_DK = 64
GLA_RANK = 16
GLA_TAU = 16.0
RG_WIDTH = 512
RG_BLOCKS = 8
RG_C = 8.0
S5_WIDTH = 512
S5_GROUP = 16
S5_GROUPS = 32
S5_STATE = 64
DN_HEADS = 4
DN_DK = 128
DN_DV = 128
D_FF = 2816

SEG = 256
SUB = 8
LANE = 128
HW = 512
AB_E = 6 * HW + LANE
CD_E = 5 * HW + LANE
S5_TILES = 8
S5_TL = 256
VMEM_LIMIT = 56 * 1024 * 1024


def _cparams():
    return pltpu.CompilerParams(dimension_semantics=("arbitrary",), vmem_limit_bytes=VMEM_LIMIT)


def _dot(a, b):
    return jnp.dot(a, b, preferred_element_type=F32)


def _dot_nt(a, b):
    return lax.dot_general(a, b, (((1,), (1,)), ((), ())), preferred_element_type=F32)


def _dot_tn(a, b):
    return lax.dot_general(a, b, (((0,), (0,)), ((), ())), preferred_element_type=F32)


def _hi_mid(x):
    hi = x.astype(BF16)
    mid = (x - hi.astype(F32)).astype(BF16)
    return hi, mid


def _dot_sel(sel_bf, x):
    hi = x.astype(BF16)
    r = x - hi.astype(F32)
    mid = r.astype(BF16)
    lo = (r - mid.astype(F32)).astype(BF16)
    return _dot(sel_bf, hi) + _dot(sel_bf, mid) + _dot(sel_bf, lo)


def _dot3(a, b):
    ah, am = _hi_mid(a)
    bh, bm = _hi_mid(b)
    return _dot(ah, bh) + _dot(ah, bm) + _dot(am, bh)


def _sigmoid(x):
    return 1.0 / (1.0 + jnp.exp(-x))


def _silu(x):
    return x * _sigmoid(x)


def _gelu(x):
    return 0.5 * x * (1.0 + jnp.tanh(math.sqrt(2.0 / math.pi) * (x + 0.044715 * (x * x * x))))


def _softplus(x):
    return jnp.maximum(x, 0.0) + jnp.log1p(jnp.exp(-jnp.abs(x)))


def _rms(x, w):
    return x * lax.rsqrt(jnp.mean(x * x, axis=-1, keepdims=True) + NORM_EPS) * w


def _head_rms(o, w, heads):
    parts = []
    for h in range(heads):
        oh = o[:, h * LANE:(h + 1) * LANE]
        parts.append(_rms(oh, w))
    return jnp.concatenate(parts, axis=1)


class _Cfg:
    def __init__(self, nc, nl, sl):
        self.nc, self.nl, self.sl = nc, nl, sl
        self.nseg = nc + nl * sl
        self.ntok = self.nseg * SEG

    def seg(self, i, rev):
        return (self.nseg - 1 - i) if rev else i

    def lat(self, g):
        return jnp.maximum(g - self.nc, 0)

    def seq(self, g):
        return jnp.minimum(self.lat(g) // self.sl, self.nl - 1)

    def mod_row(self, g):
        return jnp.where(g < self.nc, 0, 1 + self.seq(g))

    def is_ctx(self, g):
        return g < self.nc

    def first(self, g):
        return jnp.logical_or(g < self.nc, self.lat(g) % self.sl == 0)

    def last(self, g):
        return jnp.logical_or(g < self.nc, self.lat(g) % self.sl == self.sl - 1)

    def start(self, g, rev):
        return self.last(g) if rev else self.first(g)

    def tok(self, width, col, rev=False):
        return pl.BlockSpec((SEG, width), lambda i: (self.seg(i, rev), col))

    def halo_prev(self, width, col, rev=False):
        return pl.BlockSpec((SUB, width), lambda i: (jnp.maximum(self.seg(i, rev) * (SEG // SUB) - 1, 0), col))

    def halo_next(self, width, col, rev=False):
        nblk = self.ntok // SUB
        return pl.BlockSpec((SUB, width), lambda i: (jnp.minimum((self.seg(i, rev) + 1) * (SEG // SUB), nblk - 1), col))

    def state_in(self, shape, rev=False):
        nz = len(shape)
        return pl.BlockSpec((1,) + tuple(shape), lambda i: (self.seq(self.seg(i, rev)),) + (0,) * nz)

    def state_out(self, shape, rev=False):
        nz = len(shape)
        return pl.BlockSpec((1,) + tuple(shape), lambda i: (jnp.minimum(self.seg(i, rev), self.nc - 1),) + (0,) * nz)

    def mod(self, rev=False):
        return pl.BlockSpec((1, 6, D_MODEL), lambda i: (self.mod_row(self.seg(i, rev)), 0, 0))


def _full(shape):
    nz = len(shape)
    return pl.BlockSpec(tuple(shape), lambda i: (0,) * nz)


def _assemble_kernel(xp_ref, xs_ref, pos_ref, o_ref, *, nc):
    i = pl.program_id(0)

    @pl.when(i < nc)
    def _():
        o_ref[...] = xp_ref[...]

    @pl.when(i >= nc)
    def _():
        o_ref[...] = xs_ref[...] + pos_ref[...]


def _assemble(cfg, xp, xs, pos):
    nc, sl = cfg.nc, cfg.sl
    return pl.pallas_call(
        functools.partial(_assemble_kernel, nc=nc),
        grid=(cfg.nseg,),
        in_specs=[
            pl.BlockSpec((SEG, D_MODEL), lambda i: (jnp.minimum(i, nc - 1), 0)),
            pl.BlockSpec((SEG, D_MODEL), lambda i: (jnp.maximum(i - nc, 0), 0)),
            pl.BlockSpec((SEG, D_MODEL), lambda i: (jnp.maximum(i - nc, 0) % sl, 0)),
        ],
        out_specs=pl.BlockSpec((SEG, D_MODEL), lambda i: (i, 0)),
        out_shape=jax.ShapeDtypeStruct((cfg.ntok, D_MODEL), F32),
        compiler_params=_cparams(),
        name="assemble",
    )(xp, xs, pos)


MOD_TN = 1024


def _mod_kernel(c_ref, w_ref, b_ref, o_ref):
    c = c_ref[...]
    s = _silu(c).astype(BF16)
    o_ref[0] = _dot(s, w_ref[0].astype(BF16)) + b_ref[0]


def _modulation(cvec, w_mod, b_mod):
    nt = 6 * D_MODEL // MOD_TN
    return pl.pallas_call(
        _mod_kernel,
        grid=(DEPTH * nt,),
        in_specs=[
            pl.BlockSpec((SUB, D_MODEL), lambda i: (0, 0)),
            pl.BlockSpec((1, D_MODEL, MOD_TN), lambda i: (i // nt, 0, i % nt)),
            pl.BlockSpec((1, 1, MOD_TN), lambda i: (i // nt, 0, i % nt)),
        ],
        out_specs=pl.BlockSpec((1, SUB, MOD_TN), lambda i: (i // nt, 0, i % nt)),
        out_shape=jax.ShapeDtypeStruct((DEPTH, SUB, 6 * D_MODEL), F32),
        compiler_params=_cparams(),
        name="modulation",
    )(cvec, w_mod, b_mod.reshape(DEPTH, 1, 6 * D_MODEL))


def _inproj_kernel(x_ref, nw_ref, mod_ref, w_ref, o_ref, *, width):
    m = mod_ref[0]
    hn = _rms(x_ref[...], nw_ref[...]) * (1.0 + m[1:2]) + m[0:1]
    hb = hn.astype(BF16)
    for c0 in range(0, width, HW):
        c1 = min(c0 + HW, width)
        o_ref[:, c0:c1] = _dot(hb, w_ref[:, c0:c1])


def _inproj(cfg, x, nw, mod_l, w):
    width = w.shape[1]
    return pl.pallas_call(
        functools.partial(_inproj_kernel, width=width),
        grid=(cfg.nseg,),
        in_specs=[cfg.tok(D_MODEL, 0), _full((1, D_MODEL)), cfg.mod(), _full((D_MODEL, width))],
        out_specs=cfg.tok(width, 0),
        out_shape=jax.ShapeDtypeStruct((cfg.ntok, width), F32),
        compiler_params=_cparams(),
        name="inproj",
    )(x, nw, mod_l, w)


def _seg_masks(rev):
    r = lax.broadcasted_iota(jnp.int32, (SEG, SEG), 0)
    c = lax.broadcasted_iota(jnp.int32, (SEG, SEG), 1)
    same = jnp.right_shift(r, CHUNK_LOG2) == jnp.right_shift(c, CHUNK_LOG2)
    incl = jnp.logical_and(same, (c >= r) if rev else (c <= r))
    strict = jnp.logical_and(same, (c > r) if rev else (c < r))
    return incl, strict


def _chunk_order(rev):
    n = SEG // CHUNK
    return list(reversed(range(n))) if rev else list(range(n))


def _gla_kernel(q_ref, k_ref, v_ref, lr_ref, wg_ref, bg_ref, s0_ref, o_ref, sout_ref, st_ref, *, cfg, rev):
    g = cfg.seg(pl.program_id(0), rev)
    is_ctx = cfg.is_ctx(g)
    start = cfg.start(g, rev)

    @pl.when(jnp.logical_and(start, is_ctx))
    def _():
        st_ref[...] = jnp.zeros_like(st_ref)

    @pl.when(jnp.logical_and(start, jnp.logical_not(is_ctx)))
    def _():
        st_ref[...] = s0_ref[0]

    incl, _ = _seg_masks(rev)
    incl_bf = jnp.where(incl, 1.0, 0.0).astype(BF16)
    z = _dot(lr_ref[...].astype(BF16), wg_ref[...]) + bg_ref[...]
    la = -_softplus(-z) * (1.0 / GLA_TAU)
    for h in range(GLA_HEADS):
        ls = slice(h * LANE, (h + 1) * LANE)
        b = _dot_sel(incl_bf, la[:, ls])
        q = q_ref[:, ls] * (GLA_DK ** -0.5)
        k = k_ref[:, ls]
        v = v_ref[:, ls]
        vb = v.astype(BF16)
        q_dec = (q * jnp.exp(b)).astype(BF16)
        k_inv = (k * jnp.exp(-b)).astype(BF16)
        att = jnp.where(incl, _dot_nt(q_dec, k_inv), 0.0)
        o_intra = _dot(att.astype(BF16), vb)
        s_t = st_ref[h]
        for c in _chunk_order(rev):
            rs = slice(c * CHUNK, (c + 1) * CHUNK)
            e = (c * CHUNK) if rev else (c * CHUNK + CHUNK - 1)
            b_end = b[e:e + 1]
            k_end = (k[rs] * jnp.exp(b_end - b[rs])).astype(BF16)
            o_ref[rs, ls] = o_intra[rs] + _dot_nt(q_dec[rs], s_t.astype(BF16))
            s_t = s_t * jnp.exp(b_end) + _dot_tn(vb[rs], k_end)
        st_ref[h] = s_t

    @pl.when(is_ctx)
    def _():
        sout_ref[0] = st_ref[...]


def _gla(cfg, proj, wg, bg, s0, rev):
    st = (GLA_HEADS, GLA_DV, LANE)
    return pl.pallas_call(
        functools.partial(_gla_kernel, cfg=cfg, rev=rev),
        grid=(cfg.nseg,),
        in_specs=[cfg.tok(HW, 0, rev), cfg.tok(HW, 1, rev), cfg.tok(HW, 2, rev), cfg.tok(LANE, 6 * HW // LANE, rev),
                  _full((LANE, HW)), _full((1, HW)), cfg.state_in(st, rev)],
        out_specs=[cfg.tok(HW, 0, rev), cfg.state_out(st, rev)],
        out_shape=[jax.ShapeDtypeStruct((cfg.ntok, HW), F32), jax.ShapeDtypeStruct((cfg.nc,) + st, F32)],
        scratch_shapes=[pltpu.VMEM(st, F32)],
        compiler_params=_cparams(),
        name="gla_rev" if rev else "gla_fwd",
    )(proj, proj, proj, proj, wg, bg, s0)


def _conv_taps(xs_ref, x_ref, xp_ref, xn_ref, w, has_prev, has_next):
    xs_ref[0:SUB] = jnp.where(has_prev, xp_ref[...], 0.0)
    xs_ref[SUB:SUB + SEG] = x_ref[...]
    xs_ref[SUB + SEG:SUB + SEG + SUB] = jnp.where(has_next, xn_ref[...], 0.0)
    y = w[0:1] * xs_ref[SUB - 1:SUB - 1 + SEG]
    for j in range(1, 4):
        y = y + w[j:j + 1] * xs_ref[SUB - 1 + j:SUB - 1 + j + SEG]
    return y


def _shift_rows(x3, d, rev):
    return pltpu.roll(x3, (SUB - d) if rev else d, 1)


def _scan_real(a, b, carry, rev):
    w = a.shape[1]
    ng = SEG // SUB
    a3 = a.reshape(ng, SUB, w)
    b3 = b.reshape(ng, SUB, w)
    rid = lax.broadcasted_iota(jnp.int32, (ng, SUB, w), 1)
    for d in (1, 2, 4):
        m = (rid < SUB - d) if rev else (rid >= d)
        sa = _shift_rows(a3, d, rev)
        sb = _shift_rows(b3, d, rev)
        b3 = jnp.where(m, a3 * sb + b3, b3)
        a3 = jnp.where(m, a3 * sa, a3)
    hs = [None] * ng
    c = carry
    for gi in (reversed(range(ng)) if rev else range(ng)):
        hg = b3[gi] + a3[gi] * c
        hs[gi] = hg
        c = hg[0:1] if rev else hg[SUB - 1:SUB]
    return jnp.concatenate(hs, axis=0), c


def _rg_kernel(x_ref, xp_ref, xn_ref, cw_ref, cb_ref, wa_ref, ba_ref, wx_ref, bx_ref, lam_ref, h0_ref,
               h_ref, sout_ref, xs_ref, carry_ref, *, cfg, rev):
    g = cfg.seg(pl.program_id(0), rev)
    is_ctx = cfg.is_ctx(g)
    start = cfg.start(g, rev)

    @pl.when(jnp.logical_and(start, is_ctx))
    def _():
        carry_ref[...] = jnp.zeros_like(carry_ref)

    @pl.when(jnp.logical_and(start, jnp.logical_not(is_ctx)))
    def _():
        carry_ref[...] = jnp.broadcast_to(h0_ref[0], carry_ref.shape)

    has_prev = jnp.logical_not(cfg.first(g))
    has_next = jnp.logical_not(cfg.last(g))
    xc = _conv_taps(xs_ref, x_ref, xp_ref, xn_ref, cw_ref[...], has_prev, has_next) + cb_ref[...]
    xb = xc.astype(BF16)
    half = RG_WIDTH // 2
    za = jnp.concatenate([_dot(xb[:, :half], wa_ref[0]), _dot(xb[:, half:], wa_ref[1])], axis=1) + ba_ref[...]
    zx = jnp.concatenate([_dot(xb[:, :half], wx_ref[0]), _dot(xb[:, half:], wx_ref[1])], axis=1) + bx_ref[...]
    r = _sigmoid(za)
    gi = _sigmoid(zx)
    log_a = (-RG_C) * r * _softplus(-lam_ref[...])
    a = jnp.exp(log_a)
    bb = jnp.sqrt(-jnp.tanh(log_a) * (a * a + 1.0)) * (gi * xc)
    for t in range(RG_WIDTH // LANE):
        ls = slice(t * LANE, (t + 1) * LANE)
        h, c = _scan_real(a[:, ls], bb[:, ls], carry_ref[0:1, ls], rev)
        h_ref[:, ls] = h
        carry_ref[0:1, ls] = c

    @pl.when(is_ctx)
    def _():
        sout_ref[0] = carry_ref[0:1]


def _rg(cfg, proj, cw, cb, wa, ba, wx, bx, lam, h0, rev):
    col = 4
    half = RG_WIDTH // 2
    return pl.pallas_call(
        functools.partial(_rg_kernel, cfg=cfg, rev=rev),
        grid=(cfg.nseg,),
        in_specs=[cfg.tok(HW, col, rev), cfg.halo_prev(HW, col, rev), cfg.halo_next(HW, col, rev),
                  _full((4, RG_WIDTH)), _full((1, RG_WIDTH)), _full((2, half, half)), _full((1, RG_WIDTH)),
                  _full((2, half, half)), _full((1, RG_WIDTH)), _full((1, RG_WIDTH)), cfg.state_in((1, RG_WIDTH), rev)],
        out_specs=[cfg.tok(HW, 0, rev), cfg.state_out((1, RG_WIDTH), rev)],
        out_shape=[jax.ShapeDtypeStruct((cfg.ntok, RG_WIDTH), F32), jax.ShapeDtypeStruct((cfg.nc, 1, RG_WIDTH), F32)],
        scratch_shapes=[pltpu.VMEM((SEG + 2 * SUB, RG_WIDTH), F32), pltpu.VMEM((SUB, RG_WIDTH), F32)],
        compiler_params=_cparams(),
        name="rglru_rev" if rev else "rglru_fwd",
    )(proj, proj, proj, cw, cb, wa, ba, wx, bx, lam, h0)


def _s5_kernel(u_ref, bw_ref, cw_ref, tab_ref, h0_ref, y_ref, sout_ref, carry_ref, *, cfg, rev):
    g = cfg.seg(pl.program_id(0), rev)
    is_ctx = cfg.is_ctx(g)
    start = cfg.start(g, rev)

    @pl.when(jnp.logical_and(start, is_ctx))
    def _():
        carry_ref[...] = jnp.zeros_like(carry_ref)

    @pl.when(jnp.logical_and(start, jnp.logical_not(is_ctx)))
    def _():
        carry_ref[...] = h0_ref[0]

    ng = SEG // SUB
    order = list(reversed(range(ng))) if rev else list(range(ng))
    for m in range(S5_WIDTH // LANE):
        ub = u_ref[:, m * LANE:(m + 1) * LANE].astype(BF16)
        y = None
        for jt in (2 * m, 2 * m + 1):
            bu = _dot(ub, bw_ref[jt])
            hre = bu[:, :S5_TL].reshape(ng, SUB, S5_TL)
            him = bu[:, S5_TL:].reshape(ng, SUB, S5_TL)
            tab = tab_ref[jt]
            for kk, d in enumerate((1, 2, 4)):
                cr = tab[kk * SUB:(kk + 1) * SUB]
                ci = tab[(kk + 3) * SUB:(kk + 4) * SUB]
                sre = _shift_rows(hre, d, rev)
                sim = _shift_rows(him, d, rev)
                hre, him = hre + cr * sre - ci * sim, him + cr * sim + ci * sre
            pr = tab[6 * SUB:7 * SUB]
            pi = tab[7 * SUB:8 * SUB]
            cre = carry_ref[0, jt:jt + 1, :]
            cim = carry_ref[1, jt:jt + 1, :]
            res = [None] * ng
            ims = [None] * ng
            for gi in order:
                hr = hre[gi] + pr * cre - pi * cim
                hi = him[gi] + pr * cim + pi * cre
                res[gi] = hr
                ims[gi] = hi
                e = 0 if rev else SUB - 1
                cre = hr[e:e + 1]
                cim = hi[e:e + 1]
            carry_ref[0, jt:jt + 1, :] = cre
            carry_ref[1, jt:jt + 1, :] = cim
            hcat = jnp.concatenate([jnp.concatenate(res, axis=0), jnp.concatenate(ims, axis=0)], axis=1)
            yj = _dot(hcat.astype(BF16), cw_ref[jt])
            y = yj if y is None else y + yj
        y_ref[:, m * LANE:(m + 1) * LANE] = y

    @pl.when(is_ctx)
    def _():
        sout_ref[0] = carry_ref[...]


def _s5(cfg, proj, bw, cw, tab, h0, rev):
    st = (2, S5_TILES, S5_TL)
    return pl.pallas_call(
        functools.partial(_s5_kernel, cfg=cfg, rev=rev),
        grid=(cfg.nseg,),
        in_specs=[cfg.tok(HW, 0, rev), _full((S5_TILES, LANE, 2 * S5_TL)), _full((S5_TILES, 2 * S5_TL, LANE)),
                  _full((S5_TILES, 8 * SUB, S5_TL)), cfg.state_in(st, rev)],
        out_specs=[cfg.tok(HW, 0, rev), cfg.state_out(st, rev)],
        out_shape=[jax.ShapeDtypeStruct((cfg.ntok, S5_WIDTH), F32), jax.ShapeDtypeStruct((cfg.nc,) + st, F32)],
        scratch_shapes=[pltpu.VMEM(st, F32)],
        compiler_params=_cparams(),
        name="s5_rev" if rev else "s5_fwd",
    )(proj, bw, cw, tab, h0)


def _l2n(x):
    return x * lax.rsqrt(jnp.sum(x * x, axis=-1, keepdims=True) + 1e-6)


def _dn_kernel(q_ref, qp_ref, qn_ref, k_ref, kp_ref, kn_ref, v_ref, vp_ref, vn_ref, ab_ref, cw_ref, hp_ref, s0_ref,
               o_ref, sout_ref, xs_ref, st_ref, *, cfg, rev):
    g = cfg.seg(pl.program_id(0), rev)
    is_ctx = cfg.is_ctx(g)
    start = cfg.start(g, rev)

    @pl.when(jnp.logical_and(start, is_ctx))
    def _():
        st_ref[...] = jnp.zeros_like(st_ref)

    @pl.when(jnp.logical_and(start, jnp.logical_not(is_ctx)))
    def _():
        st_ref[...] = s0_ref[0]

    has_prev = jnp.logical_not(cfg.first(g))
    has_next = jnp.logical_not(cfg.last(g))
    cw = cw_ref[...]
    qc = _silu(_conv_taps(xs_ref, q_ref, qp_ref, qn_ref, cw[:, 0:HW], has_prev, has_next))
    kc = _silu(_conv_taps(xs_ref, k_ref, kp_ref, kn_ref, cw[:, HW:2 * HW], has_prev, has_next))
    vc = _silu(_conv_taps(xs_ref, v_ref, vp_ref, vn_ref, cw[:, 2 * HW:3 * HW], has_prev, has_next))
    ab = ab_ref[...]
    hp = hp_ref[...]
    d = 1 if rev else 0
    incl, strict = _seg_masks(rev)
    incl_bf = jnp.where(incl, 1.0, 0.0).astype(BF16)
    r_i = lax.broadcasted_iota(jnp.int32, (SEG, SEG), 0)
    c_i = lax.broadcasted_iota(jnp.int32, (SEG, SEG), 1)
    eye = jnp.where(r_i == c_i, 1.0, 0.0)
    same = [jnp.right_shift(r_i, lg) == jnp.right_shift(c_i, lg) for lg in range(DN_BASE_LOG2, CHUNK_LOG2 + 1)]
    base_mask = jnp.logical_and(strict, same[0])
    level_masks = [jnp.logical_and(strict, jnp.logical_and(same[i + 1], jnp.logical_not(same[i])))
                   for i in range(CHUNK_LOG2 - DN_BASE_LOG2)]
    for h in range(DN_HEADS):
        ls = slice(h * LANE, (h + 1) * LANE)
        ci = d * DN_HEADS + h
        la_col = hp[0:1, ci:ci + 1] * _softplus(ab[:, ci:ci + 1] + hp[1:2, ci:ci + 1])
        beta = _sigmoid(ab[:, 2 * DN_HEADS + ci:2 * DN_HEADS + ci + 1])
        g_col = _dot_sel(incl_bf, jnp.broadcast_to(la_col, (SEG, SEG)))
        g_row = g_col.T
        dec = jnp.where(incl, jnp.exp(g_col - g_row), 0.0)
        q = _l2n(qc[:, ls]) * (DN_DK ** -0.5)
        k = _l2n(kc[:, ls])
        v = vc[:, ls]
        kb = k.astype(BF16)
        kk = _dot_nt(kb, kb)
        aval = beta * kk * dec
        n = jnp.where(base_mask, -aval, 0.0)
        m_acc = eye + n
        p = _dot3(n, n)
        mp = _dot3(jnp.concatenate([m_acc, p], axis=0), p)
        m_acc = m_acc + mp[:SEG]
        inv = m_acc + _dot3(m_acc, mp[SEG:])
        for lm in level_masks:
            inv = inv - _dot3(inv, _dot3(jnp.where(lm, aval, 0.0), inv))
        eg = jnp.exp(g_col[:, :LANE])
        rhs = jnp.concatenate([beta * v, (beta * eg) * k], axis=1)
        sol = _dot3(inv, rhs)
        u_pre = sol[:, :DN_DV]
        w_pre = sol[:, DN_DV:].astype(BF16)
        qk = _dot_nt(q.astype(BF16), kb) * dec
        q_dec = (q * eg).astype(BF16)
        s = st_ref[h]
        for c in _chunk_order(rev):
            rs = slice(c * CHUNK, (c + 1) * CHUNK)
            e = (c * CHUNK) if rev else (c * CHUNK + CHUNK - 1)
            g_end = g_col[e:e + 1, :LANE]
            k_end = (k[rs] * jnp.exp(g_end - g_col[rs, :LANE])).astype(BF16)
            sb = s.astype(BF16)
            w_new = u_pre[rs] - _dot(w_pre[rs], sb)
            wnb = w_new.astype(BF16)
            o_ref[rs, ls] = _dot(q_dec[rs], sb) + _dot(qk[rs, rs].astype(BF16), wnb)
            s = s * jnp.exp(g_end) + _dot_tn(k_end, wnb)
        st_ref[h] = s

    @pl.when(is_ctx)
    def _():
        sout_ref[0] = st_ref[...]


def _dn(cfg, proj, cw, hp, s0, rev):
    st = (DN_HEADS, DN_DK, DN_DV)
    tok3 = []
    for col in (1, 2, 3):
        tok3 += [cfg.tok(HW, col, rev), cfg.halo_prev(HW, col, rev), cfg.halo_next(HW, col, rev)]
    return pl.pallas_call(
        functools.partial(_dn_kernel, cfg=cfg, rev=rev),
        grid=(cfg.nseg,),
        in_specs=tok3 + [cfg.tok(LANE, 5 * HW // LANE, rev), _full((4, 3 * HW)), _full((2, LANE)), cfg.state_in(st, rev)],
        out_specs=[cfg.tok(HW, 0, rev), cfg.state_out(st, rev)],
        out_shape=[jax.ShapeDtypeStruct((cfg.ntok, HW), F32), jax.ShapeDtypeStruct((cfg.nc,) + st, F32)],
        scratch_shapes=[pltpu.VMEM((SEG + 2 * SUB, HW), F32), pltpu.VMEM(st, F32)],
        compiler_params=_cparams(),
        name="dn_rev" if rev else "dn_fwd",
    )(*([proj] * 10), cw, hp, s0)


def _out_ab_kernel(x_ref, of_ref, ob_ref, g_ref, hf_ref, hb_ref, rg_ref, nw_ref, mod_ref, w_ref, o_ref):
    o_gla = _head_rms(of_ref[...] + ob_ref[...], nw_ref[...], GLA_HEADS) * _silu(g_ref[...])
    o_rg = _gelu(rg_ref[...]) * (hf_ref[...] + hb_ref[...])
    out = _dot(o_gla.astype(BF16), w_ref[0:HW]) + _dot(o_rg.astype(BF16), w_ref[HW:2 * HW])
    o_ref[...] = x_ref[...] + mod_ref[0][2:3] * out


def _out_ab(cfg, x, o_f, o_b, proj, h_f, h_b, nw, mod_l, w):
    return pl.pallas_call(
        _out_ab_kernel,
        grid=(cfg.nseg,),
        in_specs=[cfg.tok(D_MODEL, 0), cfg.tok(HW, 0), cfg.tok(HW, 0), cfg.tok(HW, 3), cfg.tok(HW, 0), cfg.tok(HW, 0),
                  cfg.tok(HW, 5), _full((1, LANE)), cfg.mod(), _full((2 * HW, D_MODEL))],
        out_specs=cfg.tok(D_MODEL, 0),
        out_shape=jax.ShapeDtypeStruct((cfg.ntok, D_MODEL), F32),
        compiler_params=_cparams(),
        name="out_ab",
    )(x, o_f, o_b, proj, h_f, h_b, proj, nw, mod_l, w)


def _out_cd_kernel(x_ref, yf_ref, yb_ref, u_ref, d_ref, gw_ref, gb_ref, of_ref, ob_ref, z_ref, nw_ref, mod_ref, w_ref,
                   o_ref):
    y = yf_ref[...] + yb_ref[...] + d_ref[...] * u_ref[...]
    zg = _gelu(y)
    o_s5 = zg * _sigmoid(_dot(zg.astype(BF16), gw_ref[...]) + gb_ref[...])
    o_dn = _head_rms(of_ref[...] + ob_ref[...], nw_ref[...], DN_HEADS) * _silu(z_ref[...])
    out = _dot(o_s5.astype(BF16), w_ref[0:HW]) + _dot(o_dn.astype(BF16), w_ref[HW:2 * HW])
    o_ref[...] = x_ref[...] + mod_ref[0][2:3] * out


def _out_cd(cfg, x, y_f, y_b, proj, s5_d, glu_w, glu_b, o_f, o_b, nw, mod_l, w):
    return pl.pallas_call(
        _out_cd_kernel,
        grid=(cfg.nseg,),
        in_specs=[cfg.tok(D_MODEL, 0), cfg.tok(HW, 0), cfg.tok(HW, 0), cfg.tok(HW, 0), _full((1, S5_WIDTH)),
                  _full((S5_WIDTH, S5_WIDTH)), _full((1, S5_WIDTH)), cfg.tok(HW, 0), cfg.tok(HW, 0), cfg.tok(HW, 4),
                  _full((1, LANE)), cfg.mod(), _full((2 * HW, D_MODEL))],
        out_specs=cfg.tok(D_MODEL, 0),
        out_shape=jax.ShapeDtypeStruct((cfg.ntok, D_MODEL), F32),
        compiler_params=_cparams(),
        name="out_cd",
    )(x, y_f, y_b, proj, s5_d, glu_w, glu_b, o_f, o_b, proj, nw, mod_l, w)


FF_TN = 256


def _ffn_kernel(x_ref, nw_ref, mod_ref, wg_ref, wu_ref, wd_ref, fnw_ref, o_ref, *, final):
    m = mod_ref[0]
    x = x_ref[...]
    hb = (_rms(x, nw_ref[...]) * (1.0 + m[4:5]) + m[3:4]).astype(BF16)
    acc = jnp.zeros((SEG, D_MODEL), F32)
    for c0 in range(0, D_FF, FF_TN):
        a = _silu(_dot(hb, wg_ref[:, c0:c0 + FF_TN])) * _dot(hb, wu_ref[:, c0:c0 + FF_TN])
        acc = acc + _dot(a.astype(BF16), wd_ref[c0:c0 + FF_TN, :])
    y = x + m[5:6] * acc
    if final:
        y = _rms(y, fnw_ref[...])
    o_ref[...] = y


def _ffn(cfg, x, nw, mod_l, wg, wu, wd, fnw, final):
    return pl.pallas_call(
        functools.partial(_ffn_kernel, final=final),
        grid=(cfg.nseg,),
        in_specs=[cfg.tok(D_MODEL, 0), _full((1, D_MODEL)), cfg.mod(), _full((D_MODEL, D_FF)), _full((D_MODEL, D_FF)),
                  _full((D_FF, D_MODEL)), _full((1, D_MODEL))],
        out_specs=cfg.tok(D_MODEL, 0),
        out_shape=jax.ShapeDtypeStruct((cfg.ntok, D_MODEL), F32),
        compiler_params=_cparams(),
        name="ffn",
    )(x, nw, mod_l, wg, wu, wd, fnw)


def _pad_heads(w, heads, dk):
    lead = w.shape[:-1]
    w = w.reshape(lead + (heads, dk))
    w = jnp.pad(w, [(0, 0)] * len(lead) + [(0, 0), (0, LANE - dk)])
    return w.reshape(lead + (heads * LANE,))


def _prep_ab_in(w):
    hk = GLA_HEADS * GLA_DK
    hv = GLA_HEADS * GLA_DV
    q, k, v, g, lr, rx, rgate = (w[:, 0:hk], w[:, hk:2 * hk], w[:, 2 * hk:2 * hk + hv], w[:, 2 * hk + hv:2 * hk + 2 * hv],
                                 w[:, 2 * hk + 2 * hv:2 * hk + 2 * hv + 2 * GLA_RANK],
                                 w[:, 2 * hk + 2 * hv + 2 * GLA_RANK:2 * hk + 2 * hv + 2 * GLA_RANK + RG_WIDTH],
                                 w[:, 2 * hk + 2 * hv + 2 * GLA_RANK + RG_WIDTH:])
    lr = jnp.pad(lr, ((0, 0), (0, LANE - 2 * GLA_RANK)))
    return jnp.concatenate([_pad_heads(q, GLA_HEADS, GLA_DK), _pad_heads(k, GLA_HEADS, GLA_DK), v, g, rx, rgate, lr],
                           axis=1).astype(BF16)


def _prep_gla_gate(wg2, bg, d):
    w = _pad_heads(wg2[d], GLA_HEADS, GLA_DK)
    w = jnp.pad(w, ((d * GLA_RANK, LANE - (d + 1) * GLA_RANK), (0, 0)))
    return w.astype(BF16), _pad_heads(bg[d], GLA_HEADS, GLA_DK)[None, :]


def _block_diag2(w):
    blk = RG_WIDTH // RG_BLOCKS
    eye = jnp.eye(RG_BLOCKS // 2, dtype=w.dtype)
    w = w.reshape(2, RG_BLOCKS // 2, blk, blk)
    full = eye[None, :, None, :, None] * w[:, :, :, None, :]
    return full.reshape(2, RG_WIDTH // 2, RG_WIDTH // 2).astype(BF16)


def _prep_s5(lam_re, lam_im, log_dt, b_re, b_im, c_re, c_im, rev):
    lam = lax.complex(lam_re, lam_im)
    dt = jnp.exp(log_dt)[:, None]
    a_bar = jnp.exp(lam * dt)
    b_bar = ((a_bar - 1.0) / lam)[..., None] * lax.complex(b_re, b_im)
    eye = jnp.eye(S5_GROUPS, dtype=F32)
    bt = jnp.transpose(b_bar, (0, 2, 1))
    b_full_re = (eye[:, None, :, None] * bt.real[:, :, None, :]).reshape(S5_WIDTH, S5_GROUPS * S5_STATE)
    b_full_im = (eye[:, None, :, None] * bt.imag[:, :, None, :]).reshape(S5_WIDTH, S5_GROUPS * S5_STATE)
    n_m = S5_WIDTH // LANE
    bre4 = b_full_re.reshape(n_m, LANE, S5_TILES, S5_TL)
    bim4 = b_full_im.reshape(n_m, LANE, S5_TILES, S5_TL)
    bw = jnp.stack([jnp.concatenate([bre4[jt // 2, :, jt, :], bim4[jt // 2, :, jt, :]], axis=1)
                    for jt in range(S5_TILES)]).astype(BF16)
    ct_re = jnp.transpose(c_re, (0, 2, 1))
    ct_im = jnp.transpose(c_im, (0, 2, 1))
    c_full_re = (eye[:, None, :, None] * ct_re[:, :, None, :]).reshape(S5_GROUPS * S5_STATE, S5_WIDTH)
    c_full_im = (eye[:, None, :, None] * ct_im[:, :, None, :]).reshape(S5_GROUPS * S5_STATE, S5_WIDTH)
    cre4 = c_full_re.reshape(S5_TILES, S5_TL, n_m, LANE)
    cim4 = c_full_im.reshape(S5_TILES, S5_TL, n_m, LANE)
    cw = jnp.stack([jnp.concatenate([cre4[jt, :, jt // 2, :], -cim4[jt, :, jt // 2, :]], axis=0)
                    for jt in range(S5_TILES)]).astype(BF16)
    ld = (lam * dt).reshape(S5_TILES, 1, S5_TL)
    rows = jnp.arange(SUB, dtype=F32)[None, :, None]
    tabs_re, tabs_im = [], []
    for dd in (1, 2, 4):
        p = jnp.exp(ld * float(dd))
        keep = (rows < SUB - dd) if rev else (rows >= dd)
        p = jnp.where(keep, jnp.broadcast_to(p, (S5_TILES, SUB, S5_TL)), 0.0)
        tabs_re.append(p.real)
        tabs_im.append(p.imag)
    expo = (SUB - rows) if rev else (rows + 1.0)
    pw = jnp.exp(ld * expo)
    tab = jnp.concatenate(tabs_re + tabs_im + [pw.real, pw.imag], axis=1).astype(F32)
    return bw, cw, tab


def _pos_embed(n_tokens):
    rows = n_tokens // GRID_W
    r, col = jnp.meshgrid(jnp.arange(rows, dtype=F32), jnp.arange(GRID_W, dtype=F32), indexing='ij')
    quarter = D_MODEL // 4
    freqs = jnp.exp(-math.log(POS_BASE) * jnp.arange(quarter, dtype=F32) / quarter)
    er = r.reshape(-1, 1) * freqs
    ec = col.reshape(-1, 1) * freqs
    return jnp.concatenate([jnp.sin(er), jnp.cos(er), jnp.sin(ec), jnp.cos(ec)], axis=-1)


def _step(x_prompt, x_sample, state_gla, state_rglru, state_s5, state_delta, c, c_ctx, p):
    nc, seq_c, _ = x_prompt.shape
    nl, seq_l, _ = x_sample.shape
    assert seq_c == SEG and seq_l % SEG == 0
    cfg = _Cfg(nc, nl, seq_l // SEG)
    x = _assemble(cfg, x_prompt.reshape(nc * SEG, D_MODEL), x_sample.reshape(nl * seq_l, D_MODEL), _pos_embed(seq_l))
    cvec = jnp.concatenate([c_ctx[None, :], c, jnp.zeros((SUB - 1 - nl, D_MODEL), F32)], axis=0)
    mod = _modulation(cvec, p['w_mod'], p['b_mod']).reshape(DEPTH, SUB, 6, D_MODEL)
    row = lambda a: a.reshape(1, -1)
    gla_out, rg_out, s5_out, dn_out = [], [], [], []
    for l in range(DEPTH):
        j = l // 2
        mod_l = mod[l]
        if l % 2 == 0:
            proj = _inproj(cfg, x, row(p['norm1_w'][l]), mod_l, _prep_ab_in(p['ab_w_in'][j]))
            o_dir, h_dir, sg_dir, sr_dir = [], [], [], []
            for d, rev in ((0, False), (1, True)):
                wg, bg = _prep_gla_gate(p['gla_wg2'][j], p['gla_bg'][j], d)
                s0 = jnp.swapaxes(state_gla[:, j, d], -1, -2)
                s0 = jnp.pad(s0, ((0, 0), (0, 0), (0, 0), (0, LANE - GLA_DK)))
                o, sg = _gla(cfg, proj, wg, bg, s0, rev)
                o_dir.append(o)
                sg_dir.append(jnp.swapaxes(sg[..., :GLA_DK], -1, -2))
                h, sr = _rg(cfg, proj, p['rg_conv_w'][j], row(p['rg_conv_b'][j]), _block_diag2(p['rg_wa'][j, d]),
                            row(p['rg_ba'][j, d]), _block_diag2(p['rg_wx'][j, d]), row(p['rg_bx'][j, d]),
                            row(p['rg_lam'][j, d]),
                            jnp.broadcast_to(state_rglru[:, j, d][:, None, :], (nl, 1, RG_WIDTH)), rev)
                h_dir.append(h)
                sr_dir.append(sr[:, 0, :])
            gla_out.append(jnp.stack(sg_dir, axis=1))
            rg_out.append(jnp.stack(sr_dir, axis=1))
            x = _out_ab(cfg, x, o_dir[0], o_dir[1], proj, h_dir[0], h_dir[1], row(p['gla_norm_w'][j]), mod_l,
                        p['ab_w_out'][j].astype(BF16))
        else:
            w_in = jnp.pad(p['cd_w_in'][j], ((0, 0), (0, CD_E - p['cd_w_in'].shape[-1]))).astype(BF16)
            proj = _inproj(cfg, x, row(p['norm1_w'][l]), mod_l, w_in)
            y_dir, o_dir, ss_dir, sd_dir = [], [], [], []
            for d, rev in ((0, False), (1, True)):
                bw, cw, tab = _prep_s5(p['s5_lam_re'][j, d], p['s5_lam_im'][j, d], p['s5_log_dt'][j, d],
                                       p['s5_b_re'][j, d], p['s5_b_im'][j, d], p['s5_c_re'][j, d], p['s5_c_im'][j, d], rev)
                h0 = jnp.moveaxis(state_s5[:, j, d], -1, 1).reshape(nl, 2, S5_TILES, S5_TL)
                y, ss = _s5(cfg, proj, bw, cw, tab, h0, rev)
                y_dir.append(y)
                ss_dir.append(jnp.moveaxis(ss.reshape(nc, 2, S5_GROUPS, S5_STATE), 1, -1))
                hp = jnp.stack([-jnp.exp(p['dn_a_log'][j].reshape(-1)), p['dn_dt_bias'][j].reshape(-1)])
                hp = jnp.pad(hp, ((0, 0), (0, LANE - 2 * DN_HEADS)))
                o, sd = _dn(cfg, proj, p['dn_conv_w'][j], hp, state_delta[:, j, d], rev)
                o_dir.append(o)
                sd_dir.append(sd)
            s5_out.append(jnp.stack(ss_dir, axis=1))
            dn_out.append(jnp.stack(sd_dir, axis=1))
            x = _out_cd(cfg, x, y_dir[0], y_dir[1], proj, row(p['s5_d'][j]), p['s5_glu_w'][j].astype(BF16),
                        row(p['s5_glu_b'][j]), o_dir[0], o_dir[1], row(p['dn_norm_w'][j]), mod_l,
                        p['cd_w_out'][j].astype(BF16))
        x = _ffn(cfg, x, row(p['norm2_w'][l]), mod_l, p['ffn_w_gate'][l].astype(BF16), p['ffn_w_up'][l].astype(BF16),
                 p['ffn_w_down'][l].astype(BF16), row(p['final_norm_w']), l == DEPTH - 1)
    y_prompt = x[:nc * SEG].reshape(nc, SEG, D_MODEL)
    y_sample = x[nc * SEG:].reshape(nl, seq_l, D_MODEL)
    return (y_prompt, y_sample, jnp.stack(gla_out, axis=1), jnp.stack(rg_out, axis=1), jnp.stack(s5_out, axis=1),
            jnp.stack(dn_out, axis=1))


def kernel(x_prompt, x_sample, state_gla, state_rglru, state_s5, state_delta, c, c_ctx, norm1_w, norm2_w, w_mod, b_mod, ab_w_in, ab_w_out, gla_wg2, gla_bg, gla_norm_w, rg_conv_w, rg_conv_b, rg_wa, rg_ba, rg_wx, rg_bx, rg_lam, cd_w_in, cd_w_out, s5_lam_re, s5_lam_im, s5_log_dt, s5_b_re, s5_b_im, s5_c_re, s5_c_im, s5_d, s5_glu_w, s5_glu_b, dn_conv_w, dn_a_log, dn_dt_bias, dn_norm_w, ffn_w_gate, ffn_w_up, ffn_w_down, final_norm_w):
    p = {
        'norm1_w': norm1_w, 'norm2_w': norm2_w, 'w_mod': w_mod, 'b_mod': b_mod,
        'ab_w_in': ab_w_in, 'ab_w_out': ab_w_out, 'gla_wg2': gla_wg2, 'gla_bg': gla_bg, 'gla_norm_w': gla_norm_w,
        'rg_conv_w': rg_conv_w, 'rg_conv_b': rg_conv_b, 'rg_wa': rg_wa, 'rg_ba': rg_ba, 'rg_wx': rg_wx,
        'rg_bx': rg_bx, 'rg_lam': rg_lam, 'cd_w_in': cd_w_in, 'cd_w_out': cd_w_out,
        's5_lam_re': s5_lam_re, 's5_lam_im': s5_lam_im, 's5_log_dt': s5_log_dt, 's5_b_re': s5_b_re,
        's5_b_im': s5_b_im, 's5_c_re': s5_c_re, 's5_c_im': s5_c_im, 's5_d': s5_d, 's5_glu_w': s5_glu_w,
        's5_glu_b': s5_glu_b, 'dn_conv_w': dn_conv_w, 'dn_a_log': dn_a_log, 'dn_dt_bias': dn_dt_bias,
        'dn_norm_w': dn_norm_w, 'ffn_w_gate': ffn_w_gate, 'ffn_w_up': ffn_w_up, 'ffn_w_down': ffn_w_down,
        'final_norm_w': final_norm_w,
    }
    return _step(x_prompt, x_sample, state_gla, state_rglru, state_s5, state_delta, c, c_ctx, p)
```
